```python
import math
import jax, jax.numpy as jnp
from jax import lax
import numpy as np

D_MODEL = 1024
BATCH = 8
SEQ = 4096
DEPTH = 1

EPS = 1e-6
NEG = -1e30
A_HEADS = 8
A_KV_HEADS = 2
A_HEAD_DIM = 64
WINDOW = 128
BLOCK = 128
N_BUCKETS = 32
MAX_DISTANCE = 128
B_HEADS = 8
Q_LORA = 384
KV_LORA = 256
NOPE_DIM = 64
ROPE_DIM = 32
V_DIM = 64
ROPE_THETA = 10000.0
A_WIDTH = A_HEADS * A_HEAD_DIM
B_WIDTH = B_HEADS * V_DIM
MIX_WIDTH = A_WIDTH + B_WIDTH
IN_SPLITS = (A_WIDTH, A_KV_HEADS * A_HEAD_DIM, A_KV_HEADS * A_HEAD_DIM, Q_LORA, KV_LORA, ROPE_DIM)
IN_WIDTH = sum(IN_SPLITS)
N_EXPERTS = 16
EXPERT_FF = 1024
CAPACITY_FACTOR = 2

kernel_name = "hybrid_swa_mla_expert_choice_block"


def rms_norm(t, g):
    tf = t.astype(jnp.float32)
    y = tf * lax.rsqrt(jnp.mean(tf * tf, axis=-1, keepdims=True) + EPS)
    return (y * g.astype(jnp.float32)).astype(t.dtype)


def t5_bucket(rel):
    half = N_BUCKETS // 2
    max_exact = half // 2
    base = jnp.where(rel > 0, half, 0)
    n = jnp.abs(rel)
    nf = jnp.maximum(n, 1).astype(jnp.float32)
    large = max_exact + (jnp.log(nf / max_exact) / math.log(MAX_DISTANCE / max_exact)
                         * (half - max_exact)).astype(jnp.int32)
    large = jnp.minimum(large, half - 1)
    return base + jnp.where(n < max_exact, n, large)


def rope(t, cos, sin):
    half = ROPE_DIM // 2
    t1, t2 = t[..., :half], t[..., half:]
    c, s = cos.astype(t.dtype), sin.astype(t.dtype)
    return jnp.concatenate([t1 * c - t2 * s, t1 * s + t2 * c], axis=-1)


def window_gqa(q, k, v, rel_bias, sink):
    bsz, seq = q.shape[0], q.shape[1]
    nb = seq // BLOCK
    g, r = A_KV_HEADS, A_HEADS // A_KV_HEADS
    qb = q.reshape(bsz, nb, BLOCK, g, r, A_HEAD_DIM)

    def band(t):
        tp = jnp.pad(t, ((0, 0), (BLOCK, BLOCK), (0, 0), (0, 0)))
        tp = tp.reshape(bsz, nb + 2, BLOCK, g, A_HEAD_DIM)
        return jnp.concatenate([tp[:, :-2], tp[:, 1:-1], tp[:, 2:]], axis=2)

    kb, vb = band(k), band(v)
    s = jnp.einsum('bnqgrd,bnkgd->bngrqk', qb, kb).astype(jnp.float32) * (A_HEAD_DIM ** -0.5)
    qi = jnp.arange(BLOCK)[:, None]
    kj = jnp.arange(3 * BLOCK)[None, :]
    rel = (kj - BLOCK) - qi
    bias = rel_bias[t5_bucket(rel)]
    bias = bias.transpose(2, 0, 1).reshape(g, r, BLOCK, 3 * BLOCK).astype(jnp.float32)
    kpos = jnp.arange(nb)[:, None] * BLOCK - BLOCK + jnp.arange(3 * BLOCK)[None, :]
    valid = (jnp.abs(rel) <= WINDOW)[None] & ((kpos >= 0) & (kpos < seq))[:, None, :]
    s = jnp.where(valid[None, :, None, None], s + bias, NEG)
    sink_logit = jnp.broadcast_to(sink.astype(jnp.float32).reshape(g, r, 1, 1), s.shape[:-1] + (1,))
    p = jax.nn.softmax(jnp.concatenate([s, sink_logit], axis=-1), axis=-1)[..., :-1]
    o = jnp.einsum('bngrqk,bnkgd->bnqgrd', p.astype(v.dtype), vb)
    return o.reshape(bsz, seq, A_WIDTH)


def latent_attention(c_q, c_kv, k_rope, cq_norm_g, w_qb, ckv_norm_g, w_kvb,
                     qn_g, qr_g, kn_g, kr_g):
    bsz, seq = c_q.shape[0], c_q.shape[1]
    nb = seq // BLOCK
    pos = jnp.arange(seq, dtype=jnp.float32)
    freqs = ROPE_THETA ** (-jnp.arange(0, ROPE_DIM, 2, dtype=jnp.float32) / ROPE_DIM)
    ang = pos[:, None] * freqs[None, :]
    cos, sin = jnp.cos(ang), jnp.sin(ang)

    q = jnp.einsum('bsc,cf->bsf', rms_norm(c_q, cq_norm_g), w_qb)
    q = q.reshape(bsz, seq, B_HEADS, NOPE_DIM + ROPE_DIM)
    q_nope = rms_norm(q[..., :NOPE_DIM], qn_g)
    q_rot = rope(rms_norm(q[..., NOPE_DIM:], qr_g), cos[:, None], sin[:, None])
    kv = jnp.einsum('bsc,cf->bsf', rms_norm(c_kv, ckv_norm_g), w_kvb)
    kv = kv.reshape(bsz, seq, B_HEADS, NOPE_DIM + V_DIM)
    k_nope = rms_norm(kv[..., :NOPE_DIM], kn_g)
    v = kv[..., NOPE_DIM:]
    k_rot = rope(rms_norm(k_rope, kr_g), cos, sin)
    scale = (NOPE_DIM + ROPE_DIM) ** -0.5

    qn_b = q_nope.reshape(bsz, nb, BLOCK, B_HEADS, NOPE_DIM).swapaxes(0, 1)
    qr_b = q_rot.reshape(bsz, nb, BLOCK, B_HEADS, ROPE_DIM).swapaxes(0, 1)

    def attend(blk):
        qn, qr = blk
        s = (jnp.einsum('bqhd,bkhd->bhqk', qn, k_nope)
             + jnp.einsum('bqhr,bkr->bhqk', qr, k_rot)).astype(jnp.float32) * scale
        p = jax.nn.softmax(s, axis=-1)
        return jnp.einsum('bhqk,bkhd->bqhd', p.astype(v.dtype), v)

    o = lax.map(attend, (qn_b, qr_b))
    return o.swapaxes(0, 1).reshape(bsz, seq, B_WIDTH)


def expert_choice_ffn(h, w_router, w_gate, w_up, w_down):
    bsz, seq, d = h.shape
    cap = CAPACITY_FACTOR * seq // N_EXPERTS
    aff = jax.nn.softmax(jnp.einsum('bsd,de->bse', h, w_router).astype(jnp.float32), axis=-1)
    gate, idx = lax.top_k(aff.transpose(0, 2, 1), cap)
    xe = jax.vmap(lambda hb, ib: hb[ib])(h, idx)
    a = jax.nn.silu(jnp.einsum('becd,edf->becf', xe, w_gate)) * jnp.einsum('becd,edf->becf', xe, w_up)
    ye = jnp.einsum('becf,efd->becd', a, w_down) * gate[..., None].astype(h.dtype)
    y = jax.vmap(lambda yb, ib: jnp.zeros((seq, d), yb.dtype).at[ib.reshape(-1)].add(yb.reshape(-1, d)))(ye, idx)
    return y


def setup_inputs(seed: int = 0) -> dict:
    key = jax.random.key(seed)
    ks = jax.random.split(key, 32)
    f32 = jnp.float32

    def w(k, shape, fan_in):
        return jax.random.normal(k, shape, f32) * (fan_in ** -0.5)

    def gain(k, shape):
        return 1.0 + 0.1 * jax.random.normal(k, shape, f32)

    L = DEPTH
    return {
        'x': jax.random.normal(ks[0], (BATCH, SEQ, D_MODEL), f32),
        'rel_bias': 0.3 * jax.random.normal(ks[1], (N_BUCKETS, A_HEADS), f32),
        'ln1_g': gain(ks[2], (L, D_MODEL)),
        'w_in': w(ks[3], (L, D_MODEL, IN_WIDTH), D_MODEL),
        'a_q_norm_g': gain(ks[4], (L, A_HEAD_DIM)),
        'a_k_norm_g': gain(ks[5], (L, A_HEAD_DIM)),
        'a_sink': 0.5 * jax.random.normal(ks[6], (L, A_HEADS), f32),
        'cq_norm_g': gain(ks[7], (L, Q_LORA)),
        'w_qb': w(ks[8], (L, Q_LORA, B_HEADS * (NOPE_DIM + ROPE_DIM)), Q_LORA),
        'ckv_norm_g': gain(ks[9], (L, KV_LORA)),
        'w_kvb': w(ks[10], (L, KV_LORA, B_HEADS * (NOPE_DIM + V_DIM)), KV_LORA),
        'b_qn_g': gain(ks[11], (L, NOPE_DIM)),
        'b_qr_g': gain(ks[12], (L, ROPE_DIM)),
        'b_kn_g': gain(ks[13], (L, NOPE_DIM)),
        'b_kr_g': gain(ks[14], (L, ROPE_DIM)),
        'out_a_g': gain(ks[15], (L, A_WIDTH)),
        'out_b_g': gain(ks[16], (L, B_WIDTH)),
        'w_o': w(ks[17], (L, MIX_WIDTH, D_MODEL), MIX_WIDTH),
        'ln2_g': gain(ks[18], (L, D_MODEL)),
        'w_router': w(ks[19], (L, D_MODEL, N_EXPERTS), D_MODEL),
        'w_gate': w(ks[20], (L, N_EXPERTS, D_MODEL, EXPERT_FF), D_MODEL),
        'w_up': w(ks[21], (L, N_EXPERTS, D_MODEL, EXPERT_FF), D_MODEL),
        'w_down': w(ks[22], (L, N_EXPERTS, EXPERT_FF, D_MODEL), EXPERT_FF),
    }


def reference(x, rel_bias, ln1_g, w_in, a_q_norm_g, a_k_norm_g, a_sink, cq_norm_g, w_qb,
              ckv_norm_g, w_kvb, b_qn_g, b_qr_g, b_kn_g, b_kr_g, out_a_g, out_b_g, w_o,
              ln2_g, w_router, w_gate, w_up, w_down):
    bsz, seq = x.shape[0], x.shape[1]
    offsets = [int(o) for o in np.cumsum(IN_SPLITS)[:-1]]
    for l in range(DEPTH):
        h = rms_norm(x, ln1_g[l])
        z = jnp.einsum('bsd,df->bsf', h, w_in[l])
        q_a, k_a, v_a, c_q, c_kv, k_rope = jnp.split(z, offsets, axis=-1)
        q_a = rms_norm(q_a.reshape(bsz, seq, A_HEADS, A_HEAD_DIM), a_q_norm_g[l])
        k_a = rms_norm(k_a.reshape(bsz, seq, A_KV_HEADS, A_HEAD_DIM), a_k_norm_g[l])
        v_a = v_a.reshape(bsz, seq, A_KV_HEADS, A_HEAD_DIM)
        o_a = window_gqa(q_a, k_a, v_a, rel_bias, a_sink[l])
        o_b = latent_attention(c_q, c_kv, k_rope, cq_norm_g[l], w_qb[l], ckv_norm_g[l], w_kvb[l],
                               b_qn_g[l], b_qr_g[l], b_kn_g[l], b_kr_g[l])
        o = jnp.concatenate([rms_norm(o_a, out_a_g[l]), rms_norm(o_b, out_b_g[l])], axis=-1)
        x = x + jnp.einsum('bsf,fd->bsd', o, w_o[l])
        h2 = rms_norm(x, ln2_g[l])
        x = x + expert_choice_ffn(h2, w_router[l], w_gate[l], w_up[l], w_down[l])
    return x
```

```python
import functools
import math

import numpy as np
import jax
import jax.numpy as jnp
from jax import lax
from jax.experimental import pallas as pl
from jax.experimental.pallas import tpu as pltpu

F32 = jnp.float32
BF16 = jnp.bfloat16

EPS = 1e-6
NEG = -1e30
LANES = 128
SUBLANES = 8
A_HEADS = 8
A_KV_HEADS = 2
A_HEAD_DIM = 64
WINDOW = 128
BLOCK = 128
N_BUCKETS = 32
MAX_DISTANCE = 128
A_WIDTH = A_HEADS * A_HEAD_DIM
B_HEADS = 8
Q_LORA = 384
KV_LORA = 256
NOPE_DIM = 64
ROPE_DIM = 32
V_DIM = 64
ROPE_THETA = 10000.0
B_WIDTH = B_HEADS * V_DIM
HEAD_SLOT = 128
N_EXPERTS = 16
CAPACITY_FACTOR = 2

TOKEN_TILE = 512
WIN_GROUP = 4
MLA_TQ = 256
MLA_KC = 512
FF_CHUNK = 256
VMEM_LIMIT = 56 * 1024 * 1024


def _cparams(sem):
    return pltpu.CompilerParams(dimension_semantics=sem, vmem_limit_bytes=VMEM_LIMIT)


def _rms_rows(x, g):
    ms = jnp.mean(x * x, axis=-1, keepdims=True)
    return x * lax.rsqrt(ms + EPS) * g


def _seg_rsqrt(x, bd):
    ms = jnp.dot((x * x).astype(BF16), bd, preferred_element_type=F32)
    return lax.rsqrt(ms + EPS)


def _proj_kernel(x_ref, ln1_ref, win_ref, gqa_ref, gka_ref, gcq_ref, wq_ref, gckv_ref, wk_ref, wv_ref,
                 cq_ref, sq_ref, ck_ref, sk_ref, gq_ref, gqsw_ref, gk_ref, gksw_ref, gkn_ref,
                 bda_ref, bdq_ref, bdk_ref, bdkn_ref,
                 qa_ref, ka_ref, kasw_ref, va_ref, vasw_ref, qb_ref, kb_ref, vb_ref, *, q_scale):
    x = x_ref[0]
    h = _rms_rows(x, ln1_ref[...]).astype(BF16)
    z = jnp.dot(h, win_ref[...], preferred_element_type=F32)
    bda = bda_ref[...]
    for i in range(A_WIDTH // LANES):
        t = z[:, i * LANES:(i + 1) * LANES]
        qa_ref[0, :, i * LANES:(i + 1) * LANES] = (t * _seg_rsqrt(t, bda) * gqa_ref[...]).astype(BF16)
    t = z[:, 512:640]
    ka_ref[0] = (t * _seg_rsqrt(t, bda) * gka_ref[...]).astype(BF16)
    t = z[:, 640:768]
    kasw_ref[0] = (t * _seg_rsqrt(t, bda) * gka_ref[...]).astype(BF16)
    va_ref[0] = z[:, 768:896].astype(BF16)
    vasw_ref[0] = z[:, 896:1024].astype(BF16)
    cqn = _rms_rows(z[:, 1024:1408], gcq_ref[...]).astype(BF16)
    qq = jnp.dot(cqn, wq_ref[...], preferred_element_type=F32)
    gc = gq_ref[...] * cq_ref[...]
    gs = gqsw_ref[...] * sq_ref[...]
    bdq = bdq_ref[...]
    width = B_HEADS * HEAD_SLOT
    for hh in range(B_HEADS):
        t = qq[:, hh * HEAD_SLOT:(hh + 1) * HEAD_SLOT]
        ts = qq[:, width + hh * HEAD_SLOT:width + (hh + 1) * HEAD_SLOT]
        r = _seg_rsqrt(t, bdq)
        qb_ref[0, :, hh * HEAD_SLOT:(hh + 1) * HEAD_SLOT] = (r * (t * gc + ts * gs) * q_scale).astype(BF16)
    ckvn = _rms_rows(z[:, 1408:1664], gckv_ref[...]).astype(BF16)
    kr = z[:, 1664:1792]
    krs = z[:, 1792:1920]
    r = _seg_rsqrt(kr, bdk_ref[...])
    krot = (r * (kr * (gk_ref[...] * ck_ref[...]) + krs * (gksw_ref[...] * sk_ref[...]))).astype(BF16)
    kin = jnp.concatenate([ckvn, krot], axis=-1)
    kp = jnp.dot(kin, wk_ref[...], preferred_element_type=F32)
    lane = lax.broadcasted_iota(jnp.int32, (1, HEAD_SLOT), 1)
    bdkn = bdkn_ref[...]
    for hh in range(B_HEADS):
        t = kp[:, hh * HEAD_SLOT:(hh + 1) * HEAD_SLOT]
        r = _seg_rsqrt(t, bdkn)
        sc = jnp.where(lane < NOPE_DIM, r * gkn_ref[...], 1.0)
        kb_ref[0, :, hh * HEAD_SLOT:(hh + 1) * HEAD_SLOT] = (t * sc).astype(BF16)
    vb_ref[0] = jnp.dot(ckvn, wv_ref[...], preferred_element_type=F32).astype(BF16)


def _block_diag(sizes_scales, n=LANES):
    m = np.zeros((n, n), np.float32)
    for lo, hi, val in sizes_scales:
        m[lo:hi, lo:hi] = val
    return jnp.asarray(m, BF16)


def _proj(x, ln1_g, w_in, a_q_norm_g, a_k_norm_g, cq_norm_g, w_qb, ckv_norm_g, w_kvb,
          b_qn_g, b_qr_g, b_kn_g, b_kr_g):
    bsz, seq, d = x.shape
    tm = TOKEN_TILE
    half = ROPE_DIM // 2
    qa_w, ka_w, va_w = w_in[:, 0:512], w_in[:, 512:640], w_in[:, 640:768]
    cq_w, ckv_w, kr_w = w_in[:, 768:1152], w_in[:, 1152:1408], w_in[:, 1408:1440]
    swap = lambda t: jnp.concatenate([t[:, 64:], t[:, :64]], axis=1)
    z96 = jnp.zeros((d, LANES - ROPE_DIM), F32)
    win = jnp.concatenate([qa_w, ka_w, swap(ka_w), va_w, swap(va_w), cq_w, ckv_w,
                           jnp.concatenate([kr_w, z96], axis=1),
                           jnp.concatenate([kr_w[:, half:], kr_w[:, :half], z96], axis=1)], axis=1).astype(BF16)
    wq3 = w_qb.reshape(Q_LORA, B_HEADS, NOPE_DIM + ROPE_DIM)
    pad = jnp.zeros((Q_LORA, B_HEADS, HEAD_SLOT - NOPE_DIM - ROPE_DIM), F32)
    wq_plain = jnp.concatenate([wq3, pad], axis=2).reshape(Q_LORA, B_HEADS * HEAD_SLOT)
    wq_partner = jnp.concatenate([jnp.zeros((Q_LORA, B_HEADS, NOPE_DIM), F32),
                                  wq3[:, :, NOPE_DIM + half:], wq3[:, :, NOPE_DIM:NOPE_DIM + half], pad],
                                 axis=2).reshape(Q_LORA, B_HEADS * HEAD_SLOT)
    wq = jnp.concatenate([wq_plain, wq_partner], axis=1).astype(BF16)
    wkv3 = w_kvb.reshape(KV_LORA, B_HEADS, NOPE_DIM + V_DIM)
    wk_top = jnp.concatenate([wkv3[:, :, :NOPE_DIM], jnp.zeros((KV_LORA, B_HEADS, HEAD_SLOT - NOPE_DIM), F32)],
                             axis=2).reshape(KV_LORA, B_HEADS * HEAD_SLOT)
    place = np.zeros((LANES, B_HEADS, HEAD_SLOT), np.float32)
    for i in range(ROPE_DIM):
        place[i, :, NOPE_DIM + i] = 1.0
    wk = jnp.concatenate([wk_top, jnp.asarray(place.reshape(LANES, B_HEADS * HEAD_SLOT))], axis=0).astype(BF16)
    vz = jnp.zeros((KV_LORA, B_HEADS // 2, V_DIM), F32)
    v3 = wkv3[:, :, NOPE_DIM:].reshape(KV_LORA, B_HEADS // 2, 2, V_DIM)
    wv = jnp.stack([jnp.concatenate([v3[:, :, 0], vz], axis=2), jnp.concatenate([vz, v3[:, :, 1]], axis=2)],
                   axis=2).reshape(KV_LORA, B_HEADS * HEAD_SLOT).astype(BF16)
    row = lambda v: v.reshape(1, -1).astype(F32)
    gqa = row(jnp.tile(a_q_norm_g, LANES // A_HEAD_DIM)) * (A_HEAD_DIM ** -0.5)
    gka = row(jnp.tile(a_k_norm_g, LANES // A_HEAD_DIM))
    zpad = jnp.zeros((HEAD_SLOT - NOPE_DIM - ROPE_DIM,), F32)
    gq = row(jnp.concatenate([b_qn_g, b_qr_g, zpad]))
    gqsw = row(jnp.concatenate([jnp.zeros((NOPE_DIM,), F32), b_qr_g[half:], b_qr_g[:half], zpad]))
    gk = row(jnp.concatenate([b_kr_g, jnp.zeros((LANES - ROPE_DIM,), F32)]))
    gksw = row(jnp.concatenate([b_kr_g[half:], b_kr_g[:half], jnp.zeros((LANES - ROPE_DIM,), F32)]))
    gkn = row(jnp.concatenate([b_kn_g, jnp.zeros((HEAD_SLOT - NOPE_DIM,), F32)]))
    pos = jnp.arange(seq, dtype=F32)
    freqs = ROPE_THETA ** (-jnp.arange(0, ROPE_DIM, 2, dtype=F32) / ROPE_DIM)
    ang = pos[:, None] * freqs[None, :]
    cos, sin = jnp.cos(ang), jnp.sin(ang)
    ones = jnp.ones((seq, NOPE_DIM), F32)
    zn = jnp.zeros((seq, NOPE_DIM), F32)
    zp = jnp.zeros((seq, HEAD_SLOT - NOPE_DIM - ROPE_DIM), F32)
    cq_tab = jnp.concatenate([ones, cos, cos, zp], axis=1)
    sq_tab = jnp.concatenate([zn, -sin, sin, zp], axis=1)
    zk = jnp.zeros((seq, LANES - ROPE_DIM), F32)
    ck_tab = jnp.concatenate([cos, cos, zk], axis=1)
    sk_tab = jnp.concatenate([-sin, sin, zk], axis=1)
    bda = _block_diag([(0, 64, 1 / 64), (64, 128, 1 / 64)])
    bdq = _block_diag([(0, 64, 1 / 64), (64, 96, 1 / 32)])
    bdk = _block_diag([(0, 32, 1 / 32)])
    bdkn = _block_diag([(0, 64, 1 / 64)])

    q_scale = (NOPE_DIM + ROPE_DIM) ** -0.5
    full = lambda a: pl.BlockSpec(a.shape, lambda b, t: (0,) * a.ndim)
    tab = pl.BlockSpec((tm, LANES), lambda b, t: (t, 0))
    tok = lambda w: pl.BlockSpec((1, tm, w), lambda b, t: (b, t, 0))
    sds = lambda w: jax.ShapeDtypeStruct((bsz, seq, w), BF16)
    ins = [x, row(ln1_g), win, gqa, gka, row(cq_norm_g), wq, row(ckv_norm_g), wk, wv,
           cq_tab, sq_tab, ck_tab, sk_tab, gq, gqsw, gk, gksw, gkn, bda, bdq, bdk, bdkn]
    specs = [tok(d)] + [full(a) for a in ins[1:10]] + [tab] * 4 + [full(a) for a in ins[14:]]
    return pl.pallas_call(
        functools.partial(_proj_kernel, q_scale=q_scale),
        grid=(bsz, seq // tm),
        in_specs=specs,
        out_specs=[tok(A_WIDTH), tok(LANES), tok(LANES), tok(LANES), tok(LANES),
                   tok(B_HEADS * HEAD_SLOT), tok(B_HEADS * HEAD_SLOT), tok(B_HEADS * HEAD_SLOT)],
        out_shape=[sds(A_WIDTH), sds(LANES), sds(LANES), sds(LANES), sds(LANES),
                   sds(B_HEADS * HEAD_SLOT), sds(B_HEADS * HEAD_SLOT), sds(B_HEADS * HEAD_SLOT)],
        compiler_params=_cparams(("arbitrary", "arbitrary")),
        name="proj",
    )(*ins)


def _t5_bucket(rel):
    half = N_BUCKETS // 2
    max_exact = half // 2
    base = jnp.where(rel > 0, half, 0)
    n = jnp.abs(rel)
    nf = jnp.maximum(n, 1).astype(F32)
    large = max_exact + (jnp.log(nf / max_exact) / math.log(MAX_DISTANCE / max_exact)
                         * (half - max_exact)).astype(jnp.int32)
    large = jnp.minimum(large, half - 1)
    return base + jnp.where(n < max_exact, n, large)


def _win_kernel(relb_ref, sink_ref, bucket_ref, qa_ref, ka_ref, kasw_ref, va_ref, vasw_ref, gout_ref,
                out_ref, bias_ref, *, nblocks):
    first = jnp.logical_and(pl.program_id(0) == 0, pl.program_id(1) == 0)

    @pl.when(first)
    def _():
        bucket = bucket_ref[...]
        for hd in range(A_HEADS):
            def body(bk, acc):
                return jnp.where(bucket == bk, relb_ref[bk, hd], acc)
            acc = lax.fori_loop(0, N_BUCKETS, body, jnp.full(bucket.shape, NEG, F32))
            bias_ref[hd] = acc

    t = pl.program_id(1)
    lane = lax.broadcasted_iota(jnp.int32, (1, LANES), 1)
    lo_mask = jnp.where(lane < A_HEAD_DIM, 1.0, 0.0).astype(BF16)
    hi_mask = jnp.where(lane >= A_HEAD_DIM, 1.0, 0.0).astype(BF16)
    col = lax.broadcasted_iota(jnp.int32, (1, 3 * BLOCK), 1)
    for qi in range(WIN_GROUP):
        n = t * WIN_GROUP + qi
        prev = jnp.maximum(n - 1, 0) * BLOCK
        own = n * BLOCK
        nxt = jnp.minimum(n + 1, nblocks - 1) * BLOCK

        def band(ref):
            return jnp.concatenate([ref[0, pl.ds(pl.multiple_of(prev, BLOCK), BLOCK), :],
                                    ref[0, pl.ds(pl.multiple_of(own, BLOCK), BLOCK), :],
                                    ref[0, pl.ds(pl.multiple_of(nxt, BLOCK), BLOCK), :]], axis=0)

        k_pl, k_sw, v_pl, v_sw = band(ka_ref), band(kasw_ref), band(va_ref), band(vasw_ref)
        edge = (jnp.where(jnp.logical_and(col < BLOCK, n == 0), NEG, 0.0)
                + jnp.where(jnp.logical_and(col >= 2 * BLOCK, n == nblocks - 1), NEG, 0.0))
        outs = []
        for pair in range(A_HEADS // 2):
            g = pair // 2
            lo_src, hi_src = (k_pl, k_sw) if g == 0 else (k_sw, k_pl)
            kk = jnp.concatenate([lo_src * lo_mask, hi_src * hi_mask], axis=0)
            lo_src, hi_src = (v_pl, v_sw) if g == 0 else (v_sw, v_pl)
            vv = jnp.concatenate([lo_src * lo_mask, hi_src * hi_mask], axis=0)
            q = qa_ref[0, qi * BLOCK:(qi + 1) * BLOCK, pair * LANES:(pair + 1) * LANES]
            s = lax.dot_general(q, kk, (((1,), (1,)), ((), ())), preferred_element_type=F32)
            ps, inv = [], []
            for j in range(2):
                hd = 2 * pair + j
                sj = s[:, j * 3 * BLOCK:(j + 1) * 3 * BLOCK] + bias_ref[hd] + edge
                sink = sink_ref[hd]
                m = jnp.maximum(jnp.max(sj, axis=-1, keepdims=True), sink)
                p = jnp.exp(sj - m)
                den = jnp.sum(p, axis=-1, keepdims=True) + jnp.exp(sink - m)
                ps.append(p.astype(BF16))
                inv.append(1.0 / den)
            o = jnp.dot(jnp.concatenate(ps, axis=1), vv, preferred_element_type=F32)
            outs.append(o * jnp.where(lane < A_HEAD_DIM, inv[0], inv[1]))
        oa = jnp.concatenate(outs, axis=1)
        out_ref[0, qi * BLOCK:(qi + 1) * BLOCK, :] = _rms_rows(oa, gout_ref[...]).astype(BF16)


def _window_attention(qa, ka, kasw, va, vasw, rel_bias, a_sink, out_a_g):
    bsz, seq, _ = qa.shape
    nblocks = seq // BLOCK
    qi = jnp.arange(BLOCK)[:, None]
    kj = jnp.arange(3 * BLOCK)[None, :]
    rel = (kj - BLOCK) - qi
    bucket = jnp.where(jnp.abs(rel) <= WINDOW, _t5_bucket(rel), -1).astype(jnp.int32)
    rows = WIN_GROUP * BLOCK
    smem = pl.BlockSpec(memory_space=pltpu.SMEM)
    seqspec = pl.BlockSpec((1, seq, LANES), lambda b, t: (b, 0, 0))
    return pl.pallas_call(
        functools.partial(_win_kernel, nblocks=nblocks),
        grid=(bsz, seq // rows),
        in_specs=[smem, smem, pl.BlockSpec(bucket.shape, lambda b, t: (0, 0)),
                  pl.BlockSpec((1, rows, A_WIDTH), lambda b, t: (b, t, 0)),
                  seqspec, seqspec, seqspec, seqspec,
                  pl.BlockSpec((1, A_WIDTH), lambda b, t: (0, 0))],
        out_specs=pl.BlockSpec((1, rows, A_WIDTH), lambda b, t: (b, t, 0)),
        out_shape=jax.ShapeDtypeStruct((bsz, seq, A_WIDTH), BF16),
        scratch_shapes=[pltpu.VMEM((A_HEADS, BLOCK, 3 * BLOCK), F32)],
        compiler_params=_cparams(("arbitrary", "arbitrary")),
        name="win",
    )(rel_bias.astype(F32), a_sink.astype(F32), bucket, qa, ka, kasw, va, vasw,
      out_a_g.reshape(1, -1).astype(F32))


def _mla_kernel(q_ref, k_ref, v_ref, out_ref, s_ref):
    seq = k_ref.shape[1]
    nchunks = seq // MLA_KC
    o = jnp.zeros((MLA_TQ, LANES), F32)
    for hh in range(2):
        sl = slice(hh * HEAD_SLOT, (hh + 1) * HEAD_SLOT)
        q = q_ref[0, :, sl]
        mx = jnp.full((MLA_TQ, LANES), -jnp.inf, F32)
        for c in range(nchunks):
            kc = k_ref[0, c * MLA_KC:(c + 1) * MLA_KC, sl]
            sc = lax.dot_general(q, kc, (((1,), (1,)), ((), ())), preferred_element_type=F32)
            s_ref[:, c * MLA_KC:(c + 1) * MLA_KC] = sc
            for i in range(MLA_KC // LANES):
                mx = jnp.maximum(mx, sc[:, i * LANES:(i + 1) * LANES])
        m = jnp.max(mx, axis=-1, keepdims=True)
        den = jnp.zeros((MLA_TQ, LANES), F32)
        acc = jnp.zeros((MLA_TQ, LANES), F32)
        for c in range(nchunks):
            p = jnp.exp(s_ref[:, c * MLA_KC:(c + 1) * MLA_KC] - m)
            for i in range(MLA_KC // LANES):
                den = den + p[:, i * LANES:(i + 1) * LANES]
            acc = acc + jnp.dot(p.astype(BF16), v_ref[0, c * MLA_KC:(c + 1) * MLA_KC, sl],
                                preferred_element_type=F32)
        o = o + acc * (1.0 / jnp.sum(den, axis=-1, keepdims=True))
    out_ref[0] = o


def _latent_attention(qb, kb, vb):
    bsz, seq, _ = qb.shape
    pairs = B_HEADS // 2
    kv_spec = pl.BlockSpec((1, seq, 2 * HEAD_SLOT), lambda b, p, t: (b, 0, p))
    return pl.pallas_call(
        _mla_kernel,
        grid=(bsz, pairs, seq // MLA_TQ),
        in_specs=[pl.BlockSpec((1, MLA_TQ, 2 * HEAD_SLOT), lambda b, p, t: (b, t, p)), kv_spec, kv_spec],
        out_specs=pl.BlockSpec((1, MLA_TQ, LANES), lambda b, p, t: (b, t, p)),
        out_shape=jax.ShapeDtypeStruct((bsz, seq, B_WIDTH), F32),
        scratch_shapes=[pltpu.VMEM((MLA_TQ, seq), F32)],
        compiler_params=_cparams(("arbitrary", "arbitrary", "arbitrary")),
        name="mla",
    )(qb, kb, vb)


def _split_bf16(v):
    hi = v.astype(BF16)
    lo = (v - hi.astype(F32)).astype(BF16)
    return hi, lo


def _mix_kernel(oa_ref, ob_ref, x_ref, wo_ref, gb_ref, ln2_ref, wrt_ref, x1_ref, h2p_ref, aff_ref):
    obn = _rms_rows(ob_ref[0], gb_ref[...]).astype(BF16)
    half = A_WIDTH
    o = (jnp.dot(oa_ref[0], wo_ref[:half, :], preferred_element_type=F32)
         + jnp.dot(obn, wo_ref[half:, :], preferred_element_type=F32))
    x1 = x_ref[0] + o
    x1_ref[0] = x1
    h2 = _rms_rows(x1, ln2_ref[...])
    d2 = h2.shape[1] // 2
    rounded = h2.astype(BF16).astype(F32)
    lo_bits = pltpu.bitcast(rounded[:, :d2], jnp.uint32)
    hi_bits = pltpu.bitcast(rounded[:, d2:], jnp.uint32)
    h2p_ref[0] = hi_bits | (lo_bits >> 16)
    h_hi, h_lo = _split_bf16(h2)
    w_hi, w_lo = _split_bf16(wrt_ref[...])
    nt = (((1,), (1,)), ((), ()))
    logits = (lax.dot_general(w_hi, h_hi, nt, preferred_element_type=F32)
              + lax.dot_general(w_lo, h_hi, nt, preferred_element_type=F32)
              + lax.dot_general(w_hi, h_lo, nt, preferred_element_type=F32))
    m = jnp.max(logits, axis=0, keepdims=True)
    ex = jnp.exp(logits - m)
    aff_ref[0] = ex / jnp.sum(ex, axis=0, keepdims=True)


def _mix(oa, ob, x, w_o, out_b_g, ln2_g, w_router):
    bsz, seq, d = x.shape
    tm = TOKEN_TILE
    tok = lambda w: pl.BlockSpec((1, tm, w), lambda b, t: (b, t, 0))
    full = lambda a: pl.BlockSpec(a.shape, lambda b, t: (0,) * a.ndim)
    ins = [oa, ob, x, w_o.astype(BF16), out_b_g.reshape(1, -1).astype(F32), ln2_g.reshape(1, -1).astype(F32),
           w_router.T.astype(F32)]
    return pl.pallas_call(
        _mix_kernel,
        grid=(bsz, seq // tm),
        in_specs=[tok(A_WIDTH), tok(B_WIDTH), tok(d)] + [full(a) for a in ins[3:]],
        out_specs=[tok(d), tok(d // 2), pl.BlockSpec((1, N_EXPERTS, tm), lambda b, t: (b, 0, t))],
        out_shape=[jax.ShapeDtypeStruct((bsz, seq, d), F32),
                   jax.ShapeDtypeStruct((bsz, seq, d // 2), jnp.uint32),
                   jax.ShapeDtypeStruct((bsz, N_EXPERTS, seq), F32)],
        compiler_params=_cparams(("arbitrary", "arbitrary")),
        name="mix",
    )(*ins)


def _lane_cumsum(mask, tri):
    rows, seq = mask.shape
    off = jnp.zeros((rows, 1), F32)
    out = []
    for c in range(seq // LANES):
        loc = jnp.dot(mask[:, c * LANES:(c + 1) * LANES].astype(BF16), tri, preferred_element_type=F32) + off
        out.append(loc)
        off = loc[:, LANES - 1:LANES]
    return jnp.concatenate(out, axis=1)


def _route_kernel(aff_ref, tri_ref, idx_ref, gate_ref, key_ref, affc_ref, *, cap):
    a = aff_ref[0]
    seq = a.shape[1]
    bits = pltpu.bitcast(a, jnp.int32)

    def bisect(i, pref):
        cand = pref | jnp.left_shift(jnp.int32(1), 30 - i)
        cnt = jnp.sum(jnp.where(bits >= cand, 1.0, 0.0), axis=-1, keepdims=True)
        return jnp.where(cnt >= cap, cand, pref)

    thr = lax.fori_loop(0, 31, bisect, jnp.zeros((N_EXPERTS, 1), jnp.int32))
    gt = bits > thr
    eq = bits == thr
    need = cap - jnp.sum(jnp.where(gt, 1.0, 0.0), axis=-1, keepdims=True)
    tri = tri_ref[...]
    eqf = jnp.where(eq, 1.0, 0.0)
    eq_rank = _lane_cumsum(eqf, tri) - eqf
    sel = jnp.logical_or(gt, jnp.logical_and(eq, eq_rank < need))
    slot = _lane_cumsum(jnp.where(sel, 1.0, 0.0), tri) - 1.0
    key = jnp.where(sel, slot, -1.0)
    nchunk = seq // LANES
    for c in range(nchunk):
        key_ref[c] = key[:, c * LANES:(c + 1) * LANES]
        affc_ref[c] = a[:, c * LANES:(c + 1) * LANES]

    lane_i = lax.broadcasted_iota(jnp.int32, (1, LANES), 1)
    expert_row = lax.broadcasted_iota(jnp.int32, (N_EXPERTS, 1), 0)
    idx_ref[...] = jnp.zeros(idx_ref.shape, jnp.int32)
    gate_ref[...] = jnp.zeros(gate_ref.shape, F32)

    def per_expert(e, carry):
        for jb in range(cap // LANES):
            jcol = lax.broadcasted_iota(jnp.int32, (LANES, 1), 0).astype(F32) + float(jb * LANES)

            def scan(c, acc):
                acc_i, acc_g = acc
                hit = key_ref[c, pl.ds(e, 1), :] == jcol
                tok = (lane_i + c * LANES).astype(F32)
                acc_i = acc_i + jnp.where(hit, tok, 0.0)
                acc_g = acc_g + jnp.where(hit, affc_ref[c, pl.ds(e, 1), :], 0.0)
                return acc_i, acc_g

            zero = jnp.zeros((LANES, LANES), F32)
            acc_i, acc_g = lax.fori_loop(0, nchunk, scan, (zero, zero))
            js = slice(jb * LANES, (jb + 1) * LANES)
            mine = expert_row == e
            idx_ref[0, :, js] = jnp.where(mine, jnp.sum(acc_i.T, axis=0, keepdims=True).astype(jnp.int32),
                                          idx_ref[0, :, js])
            gate_ref[0, :, js] = jnp.where(mine, jnp.sum(acc_g.T, axis=0, keepdims=True), gate_ref[0, :, js])
        return carry

    lax.fori_loop(0, N_EXPERTS, per_expert, 0)


def _route(aff_t, cap):
    bsz, _, seq = aff_t.shape
    tri = jnp.asarray(np.triu(np.ones((LANES, LANES), np.float32)), BF16)
    spec_o = pl.BlockSpec((1, N_EXPERTS, cap), lambda b: (b, 0, 0))
    return pl.pallas_call(
        functools.partial(_route_kernel, cap=cap),
        grid=(bsz,),
        in_specs=[pl.BlockSpec((1, N_EXPERTS, seq), lambda b: (b, 0, 0)),
                  pl.BlockSpec((LANES, LANES), lambda b: (0, 0))],
        out_specs=[spec_o, spec_o],
        out_shape=[jax.ShapeDtypeStruct((bsz, N_EXPERTS, cap), jnp.int32),
                   jax.ShapeDtypeStruct((bsz, N_EXPERTS, cap), F32)],
        scratch_shapes=[pltpu.VMEM((seq // LANES, N_EXPERTS, LANES), F32),
                        pltpu.VMEM((seq // LANES, N_EXPERTS, LANES), F32)],
        compiler_params=_cparams(("arbitrary",)),
        name="route",
    )(aff_t, tri)


def _ffn_kernel(idx_ref, gate_ref, h2p_hbm, x1_hbm, wg_ref, wu_ref, wd_ref, out_hbm,
                h2p_v, y_v, xep_v, xe_v, a_v, ye_v, sem, *, cap):
    b = pl.program_id(0)
    e = pl.program_id(1)
    row = b * N_EXPERTS + e

    @pl.when(e == 0)
    def _():
        c1 = pltpu.make_async_copy(h2p_hbm.at[b], h2p_v, sem.at[0])
        c2 = pltpu.make_async_copy(x1_hbm.at[b], y_v, sem.at[1])
        c1.start()
        c2.start()
        c1.wait()
        c2.wait()

    def gather(jj, carry):
        base = pl.multiple_of(jj * SUBLANES, SUBLANES)
        rows = [h2p_v[pl.ds(idx_ref[row, base + r], 1), :] for r in range(SUBLANES)]
        xep_v[pl.ds(base, SUBLANES), :] = jnp.concatenate(rows, axis=0)
        return carry

    lax.fori_loop(0, cap // SUBLANES, gather, 0)
    w = xep_v[...]
    d2 = w.shape[1]
    xe_v[:, :d2] = pltpu.bitcast(w << 16, F32).astype(BF16)
    xe_v[:, d2:] = pltpu.bitcast(w & jnp.uint32(0xFFFF0000), F32).astype(BF16)
    xe = xe_v[...]
    ff = wg_ref.shape[2]
    for c in range(ff // FF_CHUNK):
        cs = slice(c * FF_CHUNK, (c + 1) * FF_CHUNK)
        g = jnp.dot(xe, wg_ref[0, :, cs], preferred_element_type=F32)
        u = jnp.dot(xe, wu_ref[0, :, cs], preferred_element_type=F32)
        a_v[:, cs] = (g * (1.0 / (1.0 + jnp.exp(-g))) * u).astype(BF16)
    act = a_v[...]
    for c in range(wd_ref.shape[2] // FF_CHUNK):
        cs = slice(c * FF_CHUNK, (c + 1) * FF_CHUNK)
        ye_v[:, cs] = jnp.dot(act, wd_ref[0, :, cs], preferred_element_type=F32)

    sub = lax.broadcasted_iota(jnp.int32, (SUBLANES, 1), 0)

    def scatter(j, carry):
        s = idx_ref[row, j]
        base = pl.multiple_of((s >> 3) << 3, SUBLANES)
        tile = y_v[pl.ds(base, SUBLANES), :]
        upd = tile + gate_ref[row, j] * ye_v[pl.ds(j, 1), :]
        y_v[pl.ds(base, SUBLANES), :] = jnp.where(sub == (s & (SUBLANES - 1)), upd, tile)
        return carry

    lax.fori_loop(0, cap, scatter, 0, unroll=4)

    @pl.when(e == N_EXPERTS - 1)
    def _():
        c3 = pltpu.make_async_copy(y_v, out_hbm.at[b], sem.at[2])
        c3.start()
        c3.wait()


def _expert_ffn(idx, gate, h2p, x1, w_gate, w_up, w_down):
    bsz, seq, d = x1.shape
    cap = idx.shape[2]
    ff = w_gate.shape[2]
    wspec = lambda k, n: pl.BlockSpec((1, k, n), lambda b, e, i, g: (e, 0, 0))
    anyspec = pl.BlockSpec(memory_space=pl.ANY)
    grid_spec = pltpu.PrefetchScalarGridSpec(
        num_scalar_prefetch=2,
        grid=(bsz, N_EXPERTS),
        in_specs=[anyspec, anyspec, wspec(d, ff), wspec(d, ff), wspec(ff, d)],
        out_specs=anyspec,
        scratch_shapes=[pltpu.VMEM((seq, d // 2), jnp.uint32),
                        pltpu.VMEM((seq, d), F32),
                        pltpu.VMEM((cap, d // 2), jnp.uint32),
                        pltpu.VMEM((cap, d), BF16),
                        pltpu.VMEM((cap, ff), BF16),
                        pltpu.VMEM((cap, d), F32),
                        pltpu.SemaphoreType.DMA((3,))],
    )
    return pl.pallas_call(
        functools.partial(_ffn_kernel, cap=cap),
        grid_spec=grid_spec,
        out_shape=jax.ShapeDtypeStruct((bsz, seq, d), F32),
        compiler_params=_cparams(("arbitrary", "arbitrary")),
        name="ffn",
    )(idx.reshape(bsz * N_EXPERTS, cap), gate.reshape(bsz * N_EXPERTS, cap), h2p, x1,
      w_gate.astype(BF16), w_up.astype(BF16), w_down.astype(BF16))


def kernel(x, rel_bias, ln1_g, w_in, a_q_norm_g, a_k_norm_g, a_sink, cq_norm_g, w_qb, ckv_norm_g, w_kvb,
           b_qn_g, b_qr_g, b_kn_g, b_kr_g, out_a_g, out_b_g, w_o, ln2_g, w_router, w_gate, w_up, w_down):
    depth = ln1_g.shape[0]
    seq = x.shape[1]
    cap = CAPACITY_FACTOR * seq // N_EXPERTS
    for l in range(depth):
        qa, ka, kasw, va, vasw, qb, kb, vb = _proj(
            x, ln1_g[l], w_in[l], a_q_norm_g[l], a_k_norm_g[l], cq_norm_g[l], w_qb[l], ckv_norm_g[l], w_kvb[l],
            b_qn_g[l], b_qr_g[l], b_kn_g[l], b_kr_g[l])
        oa = _window_attention(qa, ka, kasw, va, vasw, rel_bias, a_sink[l], out_a_g[l])
        ob = _latent_attention(qb, kb, vb)
        x1, h2p, aff_t = _mix(oa, ob, x, w_o[l], out_b_g[l], ln2_g[l], w_router[l])
        idx, gate = _route(aff_t, cap)
        x = _expert_ffn(idx, gate, h2p, x1, w_gate[l], w_up[l], w_down[l])
    return x
```

```python
import functools
import math

import numpy as np
import jax
import jax.numpy as jnp
from jax import lax
from jax.experimental import pallas as pl
from jax.experimental.pallas import tpu as pltpu

F32 = jnp.float32
BF16 = jnp.bfloat16

EPS = 1e-6
NEG = -1e30
LANES = 128
SUBLANES = 8
A_HEADS = 8
A_KV_HEADS = 2
A_HEAD_DIM = 64
WINDOW = 128
BLOCK = 128
N_BUCKETS = 32
MAX_DISTANCE = 128
A_WIDTH = A_HEADS * A_HEAD_DIM
B_HEADS = 8
Q_LORA = 384
KV_LORA = 256
NOPE_DIM = 64
ROPE_DIM = 32
V_DIM = 64
ROPE_THETA = 10000.0
B_WIDTH = B_HEADS * V_DIM
HEAD_SLOT = 128
N_EXPERTS = 16
CAPACITY_FACTOR = 2

TOKEN_TILE = 512
WIN_GROUP = 4
MLA_TQ = 512
MLA_KC = 512
FF_CHUNK = 256
VMEM_LIMIT = 56 * 1024 * 1024


def _cparams(sem):
    return pltpu.CompilerParams(dimension_semantics=sem, vmem_limit_bytes=VMEM_LIMIT)


def _rms_rows(x, g):
    ms = jnp.mean(x * x, axis=-1, keepdims=True)
    return x * lax.rsqrt(ms + EPS) * g


def _seg_rsqrt(x, bd):
    ms = jnp.dot((x * x).astype(BF16), bd, preferred_element_type=F32)
    return lax.rsqrt(ms + EPS)


def _proj_kernel(x_ref, ln1_ref, win_ref, gqa_ref, gka_ref, gcq_ref, wq_ref, gckv_ref, wk_ref, wv_ref,
                 cq_ref, sq_ref, ck_ref, sk_ref, gq_ref, gqsw_ref, gk_ref, gksw_ref, gkn_ref,
                 bda_ref, bdq_ref, bdk_ref, bdkn_ref, vone_ref,
                 qa_ref, ka_ref, kasw_ref, va_ref, vasw_ref, qb_ref, kb_ref, vb_ref, *, q_scale):
    x = x_ref[0]
    h = _rms_rows(x, ln1_ref[...]).astype(BF16)
    z = jnp.dot(h, win_ref[...], preferred_element_type=F32)
    bda = bda_ref[...]
    for i in range(A_WIDTH // LANES):
        t = z[:, i * LANES:(i + 1) * LANES]
        qa_ref[0, :, i * LANES:(i + 1) * LANES] = (t * _seg_rsqrt(t, bda) * gqa_ref[...]).astype(BF16)
    t = z[:, 512:640]
    ka_ref[0] = (t * _seg_rsqrt(t, bda) * gka_ref[...]).astype(BF16)
    t = z[:, 640:768]
    kasw_ref[0] = (t * _seg_rsqrt(t, bda) * gka_ref[...]).astype(BF16)
    va_ref[0] = z[:, 768:896].astype(BF16)
    vasw_ref[0] = z[:, 896:1024].astype(BF16)
    cqn = _rms_rows(z[:, 1024:1408], gcq_ref[...]).astype(BF16)
    qq = jnp.dot(cqn, wq_ref[...], preferred_element_type=F32)
    gc = gq_ref[...] * cq_ref[...]
    gs = gqsw_ref[...] * sq_ref[...]
    bdq = bdq_ref[...]
    width = B_HEADS * HEAD_SLOT
    for hh in range(B_HEADS):
        t = qq[:, hh * HEAD_SLOT:(hh + 1) * HEAD_SLOT]
        ts = qq[:, width + hh * HEAD_SLOT:width + (hh + 1) * HEAD_SLOT]
        r = _seg_rsqrt(t, bdq)
        qb_ref[0, :, hh * HEAD_SLOT:(hh + 1) * HEAD_SLOT] = (r * (t * gc + ts * gs) * q_scale).astype(BF16)
    ckvn = _rms_rows(z[:, 1408:1664], gckv_ref[...]).astype(BF16)
    kr = z[:, 1664:1792]
    krs = z[:, 1792:1920]
    r = _seg_rsqrt(kr, bdk_ref[...])
    krot = (r * (kr * (gk_ref[...] * ck_ref[...]) + krs * (gksw_ref[...] * sk_ref[...]))).astype(BF16)
    kin = jnp.concatenate([ckvn, krot], axis=-1)
    kp = jnp.dot(kin, wk_ref[...], preferred_element_type=F32)
    lane = lax.broadcasted_iota(jnp.int32, (1, HEAD_SLOT), 1)
    bdkn = bdkn_ref[...]
    for hh in range(B_HEADS):
        t = kp[:, hh * HEAD_SLOT:(hh + 1) * HEAD_SLOT]
        r = _seg_rsqrt(t, bdkn)
        sc = jnp.where(lane < NOPE_DIM, r * gkn_ref[...], 1.0)
        kb_ref[0, :, hh * HEAD_SLOT:(hh + 1) * HEAD_SLOT] = (t * sc).astype(BF16)
    vb_ref[0] = (jnp.dot(ckvn, wv_ref[...], preferred_element_type=F32) + vone_ref[...]).astype(BF16)


def _block_diag(sizes_scales, n=LANES):
    m = np.zeros((n, n), np.float32)
    for lo, hi, val in sizes_scales:
        m[lo:hi, lo:hi] = val
    return jnp.asarray(m, BF16)


def _proj(x, ln1_g, w_in, a_q_norm_g, a_k_norm_g, cq_norm_g, w_qb, ckv_norm_g, w_kvb,
          b_qn_g, b_qr_g, b_kn_g, b_kr_g):
    bsz, seq, d = x.shape
    tm = TOKEN_TILE
    half = ROPE_DIM // 2
    qa_w, ka_w, va_w = w_in[:, 0:512], w_in[:, 512:640], w_in[:, 640:768]
    cq_w, ckv_w, kr_w = w_in[:, 768:1152], w_in[:, 1152:1408], w_in[:, 1408:1440]
    swap = lambda t: jnp.concatenate([t[:, 64:], t[:, :64]], axis=1)
    z96 = jnp.zeros((d, LANES - ROPE_DIM), F32)
    win = jnp.concatenate([qa_w, ka_w, swap(ka_w), va_w, swap(va_w), cq_w, ckv_w,
                           jnp.concatenate([kr_w, z96], axis=1),
                           jnp.concatenate([kr_w[:, half:], kr_w[:, :half], z96], axis=1)], axis=1).astype(BF16)
    wq3 = w_qb.reshape(Q_LORA, B_HEADS, NOPE_DIM + ROPE_DIM)
    pad = jnp.zeros((Q_LORA, B_HEADS, HEAD_SLOT - NOPE_DIM - ROPE_DIM), F32)
    wq_plain = jnp.concatenate([wq3, pad], axis=2).reshape(Q_LORA, B_HEADS * HEAD_SLOT)
    wq_partner = jnp.concatenate([jnp.zeros((Q_LORA, B_HEADS, NOPE_DIM), F32),
                                  wq3[:, :, NOPE_DIM + half:], wq3[:, :, NOPE_DIM:NOPE_DIM + half], pad],
                                 axis=2).reshape(Q_LORA, B_HEADS * HEAD_SLOT)
    wq = jnp.concatenate([wq_plain, wq_partner], axis=1).astype(BF16)
    wkv3 = w_kvb.reshape(KV_LORA, B_HEADS, NOPE_DIM + V_DIM)
    wk_top = jnp.concatenate([wkv3[:, :, :NOPE_DIM], jnp.zeros((KV_LORA, B_HEADS, HEAD_SLOT - NOPE_DIM), F32)],
                             axis=2).reshape(KV_LORA, B_HEADS * HEAD_SLOT)
    place = np.zeros((LANES, B_HEADS, HEAD_SLOT), np.float32)
    for i in range(ROPE_DIM):
        place[i, :, NOPE_DIM + i] = 1.0
    wk = jnp.concatenate([wk_top, jnp.asarray(place.reshape(LANES, B_HEADS * HEAD_SLOT))], axis=0).astype(BF16)
    vz = jnp.zeros((KV_LORA, B_HEADS // 2, V_DIM), F32)
    v3 = wkv3[:, :, NOPE_DIM:].reshape(KV_LORA, B_HEADS // 2, 2, V_DIM)
    wv = jnp.stack([jnp.concatenate([v3[:, :, 0], vz], axis=2), jnp.concatenate([vz, v3[:, :, 1]], axis=2)],
                   axis=2).reshape(KV_LORA, B_HEADS * HEAD_SLOT).astype(BF16)
    row = lambda v: v.reshape(1, -1).astype(F32)
    gqa = row(jnp.tile(a_q_norm_g, LANES // A_HEAD_DIM)) * (A_HEAD_DIM ** -0.5)
    gka = row(jnp.tile(a_k_norm_g, LANES // A_HEAD_DIM))
    zpad = jnp.zeros((HEAD_SLOT - NOPE_DIM - ROPE_DIM,), F32)
    gq = row(jnp.concatenate([b_qn_g, b_qr_g, zpad]))
    gqsw = row(jnp.concatenate([jnp.zeros((NOPE_DIM,), F32), b_qr_g[half:], b_qr_g[:half], zpad]))
    gk = row(jnp.concatenate([b_kr_g, jnp.zeros((LANES - ROPE_DIM,), F32)]))
    gksw = row(jnp.concatenate([b_kr_g[half:], b_kr_g[:half], jnp.zeros((LANES - ROPE_DIM,), F32)]))
    gkn = row(jnp.concatenate([b_kn_g, jnp.zeros((HEAD_SLOT - NOPE_DIM,), F32)]))
    pos = jnp.arange(seq, dtype=F32)
    freqs = ROPE_THETA ** (-jnp.arange(0, ROPE_DIM, 2, dtype=F32) / ROPE_DIM)
    ang = pos[:, None] * freqs[None, :]
    cos, sin = jnp.cos(ang), jnp.sin(ang)
    ones = jnp.ones((seq, NOPE_DIM), F32)
    zn = jnp.zeros((seq, NOPE_DIM), F32)
    zp = jnp.zeros((seq, HEAD_SLOT - NOPE_DIM - ROPE_DIM), F32)
    cq_tab = jnp.concatenate([ones, cos, cos, zp], axis=1)
    sq_tab = jnp.concatenate([zn, -sin, sin, zp], axis=1)
    zk = jnp.zeros((seq, LANES - ROPE_DIM), F32)
    ck_tab = jnp.concatenate([cos, cos, zk], axis=1)
    sk_tab = jnp.concatenate([-sin, sin, zk], axis=1)
    bda = _block_diag([(0, 64, 1 / 64), (64, 128, 1 / 64)])
    bdq = _block_diag([(0, 64, 1 / 64), (64, 96, 1 / 32)])
    bdk = _block_diag([(0, 32, 1 / 32)])
    bdkn = _block_diag([(0, 64, 1 / 64)])

    vone = np.zeros((B_HEADS, HEAD_SLOT), np.float32)
    vone[0::2, V_DIM] = 1.0
    vone[1::2, 0] = 1.0
    vone = jnp.asarray(vone.reshape(1, -1))
    q_scale = (NOPE_DIM + ROPE_DIM) ** -0.5 * math.log2(math.e)
    full = lambda a: pl.BlockSpec(a.shape, lambda b, t: (0,) * a.ndim)
    tab = pl.BlockSpec((tm, LANES), lambda b, t: (t, 0))
    tok = lambda w: pl.BlockSpec((1, tm, w), lambda b, t: (b, t, 0))
    sds = lambda w: jax.ShapeDtypeStruct((bsz, seq, w), BF16)
    ins = [x, row(ln1_g), win, gqa, gka, row(cq_norm_g), wq, row(ckv_norm_g), wk, wv,
           cq_tab, sq_tab, ck_tab, sk_tab, gq, gqsw, gk, gksw, gkn, bda, bdq, bdk, bdkn, vone]
    specs = [tok(d)] + [full(a) for a in ins[1:10]] + [tab] * 4 + [full(a) for a in ins[14:]]
    return pl.pallas_call(
        functools.partial(_proj_kernel, q_scale=q_scale),
        grid=(bsz, seq // tm),
        in_specs=specs,
        out_specs=[tok(A_WIDTH), tok(LANES), tok(LANES), tok(LANES), tok(LANES),
                   tok(B_HEADS * HEAD_SLOT), tok(B_HEADS * HEAD_SLOT), tok(B_HEADS * HEAD_SLOT)],
        out_shape=[sds(A_WIDTH), sds(LANES), sds(LANES), sds(LANES), sds(LANES),
                   sds(B_HEADS * HEAD_SLOT), sds(B_HEADS * HEAD_SLOT), sds(B_HEADS * HEAD_SLOT)],
        compiler_params=_cparams(("arbitrary", "arbitrary")),
        name="proj",
    )(*ins)


def _t5_bucket(rel):
    half = N_BUCKETS // 2
    max_exact = half // 2
    base = jnp.where(rel > 0, half, 0)
    n = jnp.abs(rel)
    nf = jnp.maximum(n, 1).astype(F32)
    large = max_exact + (jnp.log(nf / max_exact) / math.log(MAX_DISTANCE / max_exact)
                         * (half - max_exact)).astype(jnp.int32)
    large = jnp.minimum(large, half - 1)
    return base + jnp.where(n < max_exact, n, large)


def _win_kernel(relb_ref, sink_ref, bucket_ref, qa_ref, ka_ref, kasw_ref, va_ref, vasw_ref, gout_ref,
                out_ref, bias_ref, *, nblocks):
    first = jnp.logical_and(pl.program_id(0) == 0, pl.program_id(1) == 0)

    @pl.when(first)
    def _():
        bucket = bucket_ref[...]
        for hd in range(A_HEADS):
            def body(bk, acc):
                return jnp.where(bucket == bk, relb_ref[bk, hd], acc)
            acc = lax.fori_loop(0, N_BUCKETS, body, jnp.full(bucket.shape, NEG, F32))
            bias_ref[hd] = acc

    t = pl.program_id(1)
    lane = lax.broadcasted_iota(jnp.int32, (1, LANES), 1)
    lo_mask = jnp.where(lane < A_HEAD_DIM, 1.0, 0.0).astype(BF16)
    hi_mask = jnp.where(lane >= A_HEAD_DIM, 1.0, 0.0).astype(BF16)
    col = lax.broadcasted_iota(jnp.int32, (1, 3 * BLOCK), 1)
    for qi in range(WIN_GROUP):
        n = t * WIN_GROUP + qi
        prev = jnp.maximum(n - 1, 0) * BLOCK
        own = n * BLOCK
        nxt = jnp.minimum(n + 1, nblocks - 1) * BLOCK

        def band(ref):
            return jnp.concatenate([ref[0, pl.ds(pl.multiple_of(prev, BLOCK), BLOCK), :],
                                    ref[0, pl.ds(pl.multiple_of(own, BLOCK), BLOCK), :],
                                    ref[0, pl.ds(pl.multiple_of(nxt, BLOCK), BLOCK), :]], axis=0)

        k_pl, k_sw, v_pl, v_sw = band(ka_ref), band(kasw_ref), band(va_ref), band(vasw_ref)
        edge = (jnp.where(jnp.logical_and(col < BLOCK, n == 0), NEG, 0.0)
                + jnp.where(jnp.logical_and(col >= 2 * BLOCK, n == nblocks - 1), NEG, 0.0))
        outs = []
        for pair in range(A_HEADS // 2):
            g = pair // 2
            lo_src, hi_src = (k_pl, k_sw) if g == 0 else (k_sw, k_pl)
            kk = jnp.concatenate([lo_src * lo_mask, hi_src * hi_mask], axis=0)
            lo_src, hi_src = (v_pl, v_sw) if g == 0 else (v_sw, v_pl)
            vv = jnp.concatenate([lo_src * lo_mask, hi_src * hi_mask], axis=0)
            q = qa_ref[0, qi * BLOCK:(qi + 1) * BLOCK, pair * LANES:(pair + 1) * LANES]
            s = lax.dot_general(q, kk, (((1,), (1,)), ((), ())), preferred_element_type=F32)
            ps, inv = [], []
            for j in range(2):
                hd = 2 * pair + j
                sj = s[:, j * 3 * BLOCK:(j + 1) * 3 * BLOCK] + bias_ref[hd] + edge
                sink = sink_ref[hd]
                m = jnp.maximum(jnp.max(sj, axis=-1, keepdims=True), sink)
                p = jnp.exp(sj - m)
                den = jnp.sum(p, axis=-1, keepdims=True) + jnp.exp(sink - m)
                ps.append(p.astype(BF16))
                inv.append(1.0 / den)
            o = jnp.dot(jnp.concatenate(ps, axis=1), vv, preferred_element_type=F32)
            outs.append(o * jnp.where(lane < A_HEAD_DIM, inv[0], inv[1]))
        oa = jnp.concatenate(outs, axis=1)
        out_ref[0, qi * BLOCK:(qi + 1) * BLOCK, :] = _rms_rows(oa, gout_ref[...]).astype(BF16)


def _window_attention(qa, ka, kasw, va, vasw, rel_bias, a_sink, out_a_g):
    bsz, seq, _ = qa.shape
    nblocks = seq // BLOCK
    qi = jnp.arange(BLOCK)[:, None]
    kj = jnp.arange(3 * BLOCK)[None, :]
    rel = (kj - BLOCK) - qi
    bucket = jnp.where(jnp.abs(rel) <= WINDOW, _t5_bucket(rel), -1).astype(jnp.int32)
    rows = WIN_GROUP * BLOCK
    smem = pl.BlockSpec(memory_space=pltpu.SMEM)
    seqspec = pl.BlockSpec((1, seq, LANES), lambda b, t: (b, 0, 0))
    return pl.pallas_call(
        functools.partial(_win_kernel, nblocks=nblocks),
        grid=(bsz, seq // rows),
        in_specs=[smem, smem, pl.BlockSpec(bucket.shape, lambda b, t: (0, 0)),
                  pl.BlockSpec((1, rows, A_WIDTH), lambda b, t: (b, t, 0)),
                  seqspec, seqspec, seqspec, seqspec,
                  pl.BlockSpec((1, A_WIDTH), lambda b, t: (0, 0))],
        out_specs=pl.BlockSpec((1, rows, A_WIDTH), lambda b, t: (b, t, 0)),
        out_shape=jax.ShapeDtypeStruct((bsz, seq, A_WIDTH), BF16),
        scratch_shapes=[pltpu.VMEM((A_HEADS, BLOCK, 3 * BLOCK), F32)],
        compiler_params=_cparams(("arbitrary", "arbitrary")),
        name="win",
    )(rel_bias.astype(F32), a_sink.astype(F32), bucket, qa, ka, kasw, va, vasw,
      out_a_g.reshape(1, -1).astype(F32))


def _mla_kernel(q_ref, k_ref, v_ref, out_ref, s0_ref, s1_ref):
    seq = k_ref.shape[1]
    nchunks = seq // MLA_KC
    s_refs = (s0_ref, s1_ref)
    sls = [slice(hh * HEAD_SLOT, (hh + 1) * HEAD_SLOT) for hh in range(2)]

    def qk_chunk(hh, c, mx):
        kc = k_ref[0, c * MLA_KC:(c + 1) * MLA_KC, sls[hh]]
        sc = lax.dot_general(q_ref[0, :, sls[hh]], kc, (((1,), (1,)), ((), ())),
                             preferred_element_type=F32)
        s_refs[hh][:, c * MLA_KC:(c + 1) * MLA_KC] = sc
        for i in range(MLA_KC // LANES):
            mx = jnp.maximum(mx, sc[:, i * LANES:(i + 1) * LANES])
        return mx

    def pv_chunk(hh, c, m, acc):
        p = jnp.exp2(s_refs[hh][:, c * MLA_KC:(c + 1) * MLA_KC] - m)
        return acc + jnp.dot(p.astype(BF16), v_ref[0, c * MLA_KC:(c + 1) * MLA_KC, sls[hh]],
                             preferred_element_type=F32)

    ninf = jnp.full((MLA_TQ, LANES), -jnp.inf, F32)
    zero = jnp.zeros((MLA_TQ, LANES), F32)
    mx0 = ninf
    for c in range(nchunks):
        mx0 = qk_chunk(0, c, mx0)
    m0 = jnp.max(mx0, axis=-1, keepdims=True)
    acc0, mx1 = zero, ninf
    for c in range(nchunks):
        acc0 = pv_chunk(0, c, m0, acc0)
        mx1 = qk_chunk(1, c, mx1)
    m1 = jnp.max(mx1, axis=-1, keepdims=True)
    acc1 = zero
    for c in range(nchunks):
        acc1 = pv_chunk(1, c, m1, acc1)
    o0 = acc0 * (1.0 / acc0[:, V_DIM:V_DIM + 1])
    o1 = acc1 * (1.0 / acc1[:, 0:1])
    lane = lax.broadcasted_iota(jnp.int32, (1, LANES), 1)
    out_ref[0] = jnp.where(lane < V_DIM, o0, o1)


def _latent_attention(qb, kb, vb):
    bsz, seq, _ = qb.shape
    pairs = B_HEADS // 2
    kv_spec = pl.BlockSpec((1, seq, 2 * HEAD_SLOT), lambda b, p, t: (b, 0, p))
    return pl.pallas_call(
        _mla_kernel,
        grid=(bsz, pairs, seq // MLA_TQ),
        in_specs=[pl.BlockSpec((1, MLA_TQ, 2 * HEAD_SLOT), lambda b, p, t: (b, t, p)), kv_spec, kv_spec],
        out_specs=pl.BlockSpec((1, MLA_TQ, LANES), lambda b, p, t: (b, t, p)),
        out_shape=jax.ShapeDtypeStruct((bsz, seq, B_WIDTH), F32),
        scratch_shapes=[pltpu.VMEM((MLA_TQ, seq), F32), pltpu.VMEM((MLA_TQ, seq), F32)],
        compiler_params=_cparams(("arbitrary", "arbitrary", "arbitrary")),
        name="mla",
    )(qb, kb, vb)


def _split_bf16(v):
    hi = v.astype(BF16)
    lo = (v - hi.astype(F32)).astype(BF16)
    return hi, lo


def _mix_kernel(oa_ref, ob_ref, x_ref, wo_ref, gb_ref, ln2_ref, wrt_ref, x1_ref, h2p_ref, aff_ref):
    obn = _rms_rows(ob_ref[0], gb_ref[...]).astype(BF16)
    half = A_WIDTH
    o = (jnp.dot(oa_ref[0], wo_ref[:half, :], preferred_element_type=F32)
         + jnp.dot(obn, wo_ref[half:, :], preferred_element_type=F32))
    x1 = x_ref[0] + o
    x1_ref[0] = x1
    h2 = _rms_rows(x1, ln2_ref[...])
    d2 = h2.shape[1] // 2
    rounded = h2.astype(BF16).astype(F32)
    lo_bits = pltpu.bitcast(rounded[:, :d2], jnp.uint32)
    hi_bits = pltpu.bitcast(rounded[:, d2:], jnp.uint32)
    h2p_ref[0] = hi_bits | (lo_bits >> 16)
    h_hi, h_lo = _split_bf16(h2)
    w_hi, w_lo = _split_bf16(wrt_ref[...])
    nt = (((1,), (1,)), ((), ()))
    logits = (lax.dot_general(w_hi, h_hi, nt, preferred_element_type=F32)
              + lax.dot_general(w_lo, h_hi, nt, preferred_element_type=F32)
              + lax.dot_general(w_hi, h_lo, nt, preferred_element_type=F32))
    m = jnp.max(logits, axis=0, keepdims=True)
    ex = jnp.exp(logits - m)
    aff_ref[0] = ex / jnp.sum(ex, axis=0, keepdims=True)


def _mix(oa, ob, x, w_o, out_b_g, ln2_g, w_router):
    bsz, seq, d = x.shape
    tm = TOKEN_TILE
    tok = lambda w: pl.BlockSpec((1, tm, w), lambda b, t: (b, t, 0))
    full = lambda a: pl.BlockSpec(a.shape, lambda b, t: (0,) * a.ndim)
    ins = [oa, ob, x, w_o.astype(BF16), out_b_g.reshape(1, -1).astype(F32), ln2_g.reshape(1, -1).astype(F32),
           w_router.T.astype(F32)]
    return pl.pallas_call(
        _mix_kernel,
        grid=(bsz, seq // tm),
        in_specs=[tok(A_WIDTH), tok(B_WIDTH), tok(d)] + [full(a) for a in ins[3:]],
        out_specs=[tok(d), tok(d // 2), pl.BlockSpec((1, N_EXPERTS, tm), lambda b, t: (b, 0, t))],
        out_shape=[jax.ShapeDtypeStruct((bsz, seq, d), F32),
                   jax.ShapeDtypeStruct((bsz, seq, d // 2), jnp.uint32),
                   jax.ShapeDtypeStruct((bsz, N_EXPERTS, seq), F32)],
        compiler_params=_cparams(("arbitrary", "arbitrary")),
        name="mix",
    )(*ins)


SLOT_LO = 32
SLOT_HI = 16
N_PARTS = 5


def _lane_cumsum(mask, tri, cmat):
    rows, seq = mask.shape
    mb = mask.astype(BF16)
    before = jnp.dot(mb, cmat, preferred_element_type=F32)
    out = []
    for c in range(seq // LANES):
        loc = jnp.dot(mb[:, c * LANES:(c + 1) * LANES], tri, preferred_element_type=F32)
        out.append(loc + before[:, c:c + 1])
    return jnp.concatenate(out, axis=1)


def _route_kernel(aff_ref, tri_ref, cmat_ref, idx_ref, gate_ref, hi_ref, lo_ref, g1_ref, g2_ref, g3_ref,
                  oh_ref, bv_ref, *, cap):
    a = aff_ref[0]
    seq = a.shape[1]
    bits = pltpu.bitcast(a, jnp.int32)

    def bisect(i, pref):
        cand = pref | jnp.left_shift(jnp.int32(1), 30 - i)
        cnt = jnp.sum(jnp.where(bits >= cand, 1.0, 0.0), axis=-1, keepdims=True)
        return jnp.where(cnt >= cap, cand, pref)

    thr = lax.fori_loop(0, 31, bisect, jnp.zeros((N_EXPERTS, 1), jnp.int32))
    gt = bits > thr
    eq = bits == thr
    need = cap - jnp.sum(jnp.where(gt, 1.0, 0.0), axis=-1, keepdims=True)
    tri = tri_ref[...]
    cmat = cmat_ref[...]
    eqf = jnp.where(eq, 1.0, 0.0)
    eq_rank = _lane_cumsum(eqf, tri, cmat) - eqf
    sel = jnp.logical_or(gt, jnp.logical_and(eq, eq_rank < need))
    slot = _lane_cumsum(jnp.where(sel, 1.0, 0.0), tri, cmat) - 1.0
    key = jnp.where(sel, slot, -1.0)
    hi = jnp.floor(key * (1.0 / SLOT_LO))
    lo = key - SLOT_LO * hi
    g1 = a.astype(BF16).astype(F32)
    g2 = (a - g1).astype(BF16).astype(F32)
    g3 = ((a - g1) - g2).astype(BF16).astype(F32)
    for c in range(seq // LANES):
        cs = slice(c * LANES, (c + 1) * LANES)
        hi_ref[c] = hi[:, cs]
        lo_ref[c] = lo[:, cs]
        g1_ref[c] = g1[:, cs]
        g2_ref[c] = g2[:, cs]
        g3_ref[c] = g3[:, cs]

    hcol = lax.broadcasted_iota(jnp.int32, (SLOT_HI, 1), 0).astype(F32)
    lcol = lax.broadcasted_iota(jnp.int32, (SLOT_LO, 1), 0).astype(F32)
    lane = lax.broadcasted_iota(jnp.int32, (1, LANES), 1).astype(F32)

    def per_expert(e, carry):
        er = pl.ds(e, 1)
        for c in range(seq // LANES):
            cs = slice(c * LANES, (c + 1) * LANES)
            oh_ref[:, cs] = jnp.where(hi_ref[c, er, :] == hcol, 1.0, 0.0).astype(BF16)
            hit = lo_ref[c, er, :] == lcol
            parts = (jnp.full((1, LANES), float(c), F32), lane, g1_ref[c, er, :], g2_ref[c, er, :],
                     g3_ref[c, er, :])
            for q, val in enumerate(parts):
                bv_ref[q * SLOT_LO:(q + 1) * SLOT_LO, cs] = jnp.where(hit, val, 0.0).astype(BF16)
        r = lax.dot_general(bv_ref[...], oh_ref[...], (((1,), (1,)), ((), ())),
                            preferred_element_type=F32)
        idx_ref[0, e] = (r[0:SLOT_LO] * float(LANES) + r[SLOT_LO:2 * SLOT_LO]).astype(jnp.int32)
        gate_ref[0, e] = r[2 * SLOT_LO:3 * SLOT_LO] + r[3 * SLOT_LO:4 * SLOT_LO] + r[4 * SLOT_LO:5 * SLOT_LO]
        return carry

    lax.fori_loop(0, N_EXPERTS, per_expert, 0)


def _route(aff_t, cap):
    bsz, _, seq = aff_t.shape
    assert cap % SLOT_LO == 0 and cap <= SLOT_LO * SLOT_HI and seq // LANES <= LANES
    tri = jnp.asarray(np.triu(np.ones((LANES, LANES), np.float32)), BF16)
    chunk = np.arange(seq)[:, None] // LANES
    cmat = jnp.asarray((chunk < np.arange(LANES)[None, :]).astype(np.float32), BF16)
    spec_o = pl.BlockSpec((1, N_EXPERTS, SLOT_LO, SLOT_HI), lambda b: (b, 0, 0, 0))
    rows = lambda dt: pltpu.VMEM((seq // LANES, N_EXPERTS, LANES), dt)
    idx_t, gate_t = pl.pallas_call(
        functools.partial(_route_kernel, cap=cap),
        grid=(bsz,),
        in_specs=[pl.BlockSpec((1, N_EXPERTS, seq), lambda b: (b, 0, 0)),
                  pl.BlockSpec((LANES, LANES), lambda b: (0, 0)),
                  pl.BlockSpec((seq, LANES), lambda b: (0, 0))],
        out_specs=[spec_o, spec_o],
        out_shape=[jax.ShapeDtypeStruct((bsz, N_EXPERTS, SLOT_LO, SLOT_HI), jnp.int32),
                   jax.ShapeDtypeStruct((bsz, N_EXPERTS, SLOT_LO, SLOT_HI), F32)],
        scratch_shapes=[rows(F32), rows(F32), rows(F32), rows(F32), rows(F32),
                        pltpu.VMEM((SLOT_HI, seq), BF16), pltpu.VMEM((N_PARTS * SLOT_LO, seq), BF16)],
        compiler_params=_cparams(("arbitrary",)),
        name="route",
    )(aff_t, tri, cmat)
    order = lambda t: t.transpose(0, 1, 3, 2)[:, :, :cap // SLOT_LO, :].reshape(bsz * N_EXPERTS, cap)
    return order(idx_t), order(gate_t)


def _ffn_kernel(idx_ref, gate_ref, h2p_hbm, x1_hbm, wg_ref, wu_ref, wd_ref, out_hbm,
                h2p_v, y_v, xep_v, xe_v, a_v, ye_v, sem, *, cap):
    b = pl.program_id(0)
    e = pl.program_id(1)
    row = b * N_EXPERTS + e

    @pl.when(e == 0)
    def _():
        c1 = pltpu.make_async_copy(h2p_hbm.at[b], h2p_v, sem.at[0])
        c2 = pltpu.make_async_copy(x1_hbm.at[b], y_v, sem.at[1])
        c1.start()
        c2.start()
        c1.wait()
        c2.wait()

    def gather(jj, carry):
        base = pl.multiple_of(jj * SUBLANES, SUBLANES)
        rows = [h2p_v[pl.ds(idx_ref[row, base + r], 1), :] for r in range(SUBLANES)]
        xep_v[pl.ds(base, SUBLANES), :] = jnp.concatenate(rows, axis=0)
        return carry

    lax.fori_loop(0, cap // SUBLANES, gather, 0)
    w = xep_v[...]
    d2 = w.shape[1]
    xe_v[:, :d2] = pltpu.bitcast(w << 16, F32).astype(BF16)
    xe_v[:, d2:] = pltpu.bitcast(w & jnp.uint32(0xFFFF0000), F32).astype(BF16)
    xe = xe_v[...]
    ff = wg_ref.shape[2]
    for c in range(ff // FF_CHUNK):
        cs = slice(c * FF_CHUNK, (c + 1) * FF_CHUNK)
        g = jnp.dot(xe, wg_ref[0, :, cs], preferred_element_type=F32)
        u = jnp.dot(xe, wu_ref[0, :, cs], preferred_element_type=F32)
        a_v[:, cs] = (g * (1.0 / (1.0 + jnp.exp(-g))) * u).astype(BF16)
    act = a_v[...]
    for c in range(wd_ref.shape[2] // FF_CHUNK):
        cs = slice(c * FF_CHUNK, (c + 1) * FF_CHUNK)
        ye_v[:, cs] = jnp.dot(act, wd_ref[0, :, cs], preferred_element_type=F32)

    sub = lax.broadcasted_iota(jnp.int32, (SUBLANES, 1), 0)

    def scatter(j, carry):
        s = idx_ref[row, j]
        base = pl.multiple_of((s >> 3) << 3, SUBLANES)
        tile = y_v[pl.ds(base, SUBLANES), :]
        upd = tile + gate_ref[row, j] * ye_v[pl.ds(j, 1), :]
        y_v[pl.ds(base, SUBLANES), :] = jnp.where(sub == (s & (SUBLANES - 1)), upd, tile)
        return carry

    lax.fori_loop(0, cap, scatter, 0, unroll=4)

    @pl.when(e == N_EXPERTS - 1)
    def _():
        c3 = pltpu.make_async_copy(y_v, out_hbm.at[b], sem.at[2])
        c3.start()
        c3.wait()


def _expert_ffn(idx, gate, h2p, x1, w_gate, w_up, w_down):
    bsz, seq, d = x1.shape
    cap = idx.shape[1]
    ff = w_gate.shape[2]
    wspec = lambda k, n: pl.BlockSpec((1, k, n), lambda b, e, i, g: (e, 0, 0))
    anyspec = pl.BlockSpec(memory_space=pl.ANY)
    grid_spec = pltpu.PrefetchScalarGridSpec(
        num_scalar_prefetch=2,
        grid=(bsz, N_EXPERTS),
        in_specs=[anyspec, anyspec, wspec(d, ff), wspec(d, ff), wspec(ff, d)],
        out_specs=anyspec,
        scratch_shapes=[pltpu.VMEM((seq, d // 2), jnp.uint32),
                        pltpu.VMEM((seq, d), F32),
                        pltpu.VMEM((cap, d // 2), jnp.uint32),
                        pltpu.VMEM((cap, d), BF16),
                        pltpu.VMEM((cap, ff), BF16),
                        pltpu.VMEM((cap, d), F32),
                        pltpu.SemaphoreType.DMA((3,))],
    )
    return pl.pallas_call(
        functools.partial(_ffn_kernel, cap=cap),
        grid_spec=grid_spec,
        out_shape=jax.ShapeDtypeStruct((bsz, seq, d), F32),
        compiler_params=_cparams(("arbitrary", "arbitrary")),
        name="ffn",
    )(idx, gate, h2p, x1,
      w_gate.astype(BF16), w_up.astype(BF16), w_down.astype(BF16))


def kernel(x, rel_bias, ln1_g, w_in, a_q_norm_g, a_k_norm_g, a_sink, cq_norm_g, w_qb, ckv_norm_g, w_kvb,
           b_qn_g, b_qr_g, b_kn_g, b_kr_g, out_a_g, out_b_g, w_o, ln2_g, w_router, w_gate, w_up, w_down):
    depth = ln1_g.shape[0]
    seq = x.shape[1]
    cap = CAPACITY_FACTOR * seq // N_EXPERTS
    for l in range(depth):
        qa, ka, kasw, va, vasw, qb, kb, vb = _proj(
            x, ln1_g[l], w_in[l], a_q_norm_g[l], a_k_norm_g[l], cq_norm_g[l], w_qb[l], ckv_norm_g[l], w_kvb[l],
            b_qn_g[l], b_qr_g[l], b_kn_g[l], b_kr_g[l])
        oa = _window_attention(qa, ka, kasw, va, vasw, rel_bias, a_sink[l], out_a_g[l])
        ob = _latent_attention(qb, kb, vb)
        x1, h2p, aff_t = _mix(oa, ob, x, w_o[l], out_b_g[l], ln2_g[l], w_router[l])
        idx, gate = _route(aff_t, cap)
        x = _expert_ffn(idx, gate, h2p, x1, w_gate[l], w_up[l], w_down[l])
    return x
```
